```python
import math
import jax
import jax.numpy as jnp
from jax import lax
import numpy as np

D_MODEL = 1024
BATCH = 32
SEQ = 2048
DEPTH = 4

GRID_W = 64
CTX_LEN = 256
N_MIXERS = 2

DA_HEADS = 8
DA_HEAD_DIM = 64
DA_V_DIM = 2 * DA_HEAD_DIM
DA_QK = DA_HEADS * 2 * DA_HEAD_DIM
DA_SCALE = DA_HEAD_DIM ** -0.5
Q_BLOCK = 128

RT_HEADS = 4
RT_QK_DIM = 256
RT_V_DIM = 512
RT_QK = RT_HEADS * RT_QK_DIM
RT_V = RT_HEADS * RT_V_DIM
RT_IN = 2 * RT_QK + 3 * RT_V
RT_SPLITS = (RT_QK, 2 * RT_QK, 2 * RT_QK + RT_V, 2 * RT_QK + 2 * RT_V)
RT_SCALE = RT_QK_DIM ** -0.5
RT_CHUNK = 128

MOE_GROUPS = 4
MOE_EXPERTS_PER_GROUP = 8
MOE_EXPERTS = MOE_GROUPS * MOE_EXPERTS_PER_GROUP
MOE_TOP_K = 2
MOE_HIDDEN = 512

ROPE_BASE = 10000.0
RMS_EPS = 1e-6

kernel_name = "hybrid_diffattn_retention_hmoe_dit"


def _rms_normalize(x):
    xf = x.astype(jnp.float32)
    return xf * lax.rsqrt(jnp.mean(xf * xf, axis=-1, keepdims=True) + RMS_EPS)


def _rms_norm(x, g):
    return (_rms_normalize(x) * g.astype(jnp.float32)).astype(x.dtype)


def _modulate(h, shift, scale):
    return h * (1.0 + scale) + shift


def _axial_rope(n_tokens, head_dim):
    rows = n_tokens // GRID_W
    t = jnp.arange(rows * GRID_W)
    row = (t // GRID_W).astype(jnp.float32)
    col = (t % GRID_W).astype(jnp.float32)
    n_freq = head_dim // 4
    inv_freq = ROPE_BASE ** (-jnp.arange(n_freq, dtype=jnp.float32) / n_freq)
    ang = jnp.concatenate([row[:, None] * inv_freq, col[:, None] * inv_freq], axis=-1)
    return jnp.cos(ang), jnp.sin(ang)


def _apply_rope(x, cos, sin):
    extra = x.ndim - 3
    c = cos.reshape(cos.shape[0], *([1] * extra), cos.shape[1])
    s = sin.reshape(sin.shape[0], *([1] * extra), sin.shape[1])
    x1, x2 = jnp.split(x.astype(jnp.float32), 2, axis=-1)
    return jnp.concatenate([x1 * c - x2 * s, x1 * s + x2 * c], axis=-1).astype(x.dtype)


def _diff_attention(h, n_ctx, w_qkv, q_g, k_g, lam, subln_g, w_o, lam_init, cos, sin, need_ctx):
    B, N, _ = h.shape
    S = N - n_ctx
    q, k, v = jnp.split(h @ w_qkv, 3, axis=-1)
    q = _rms_norm(q.reshape(B, N, DA_HEADS, 2, DA_HEAD_DIM), q_g)
    k = _rms_norm(k.reshape(B, N, DA_HEADS, 2, DA_HEAD_DIM), k_g)
    v = v.reshape(B, N, DA_HEADS, DA_V_DIM)
    q_lat = _apply_rope(q[:, n_ctx:], cos, sin)
    k = jnp.concatenate([k[:, :n_ctx], _apply_rope(k[:, n_ctx:], cos, sin)], axis=1)
    lam_f = lam.astype(jnp.float32)
    lam_full = jnp.exp(jnp.sum(lam_f[0] * lam_f[1])) - jnp.exp(jnp.sum(lam_f[2] * lam_f[3])) + lam_init

    def attend(qb, kb, vb):
        s = jnp.einsum('bqhmd,bkhmd->bhmqk', qb, kb).astype(jnp.float32) * DA_SCALE
        p = jax.nn.softmax(s, axis=-1)
        a = p[:, :, 0] - lam_full * p[:, :, 1]
        return jnp.einsum('bhqk,bkhe->bqhe', a.astype(vb.dtype), vb)

    nb = S // Q_BLOCK
    q_blocks = jnp.moveaxis(q_lat.reshape(B, nb, Q_BLOCK, DA_HEADS, 2, DA_HEAD_DIM), 1, 0)
    o_lat = lax.map(lambda qb: attend(qb, k, v), q_blocks)
    o_lat = jnp.moveaxis(o_lat, 0, 1).reshape(B, S, DA_HEADS, DA_V_DIM)
    if need_ctx:
        o_ctx = attend(q[:, :n_ctx], k[:, :n_ctx], v[:, :n_ctx])
        o = jnp.concatenate([o_ctx, o_lat], axis=1)
    else:
        o = o_lat
    o = _rms_norm(o, subln_g) * (1.0 - lam_init)
    return o.reshape(B, o.shape[1], DA_HEADS * DA_V_DIM) @ w_o


def _retention_scan(q, k, v, log_g, state0):
    B, H, n, dk = q.shape
    dv = v.shape[-1]
    nc = n // RT_CHUNK
    pos = jnp.arange(RT_CHUNK, dtype=jnp.float32)
    rel = pos[:, None] - pos[None, :]
    lg = log_g[:, None]
    decay_intra = jnp.where(rel >= 0, jnp.exp(log_g[:, None, None] * jnp.maximum(rel, 0.0)), 0.0)
    decay_q = jnp.exp(lg * (pos + 1.0))[:, :, None]
    decay_k = jnp.exp(lg * (RT_CHUNK - 1.0 - pos))[:, :, None]
    decay_chunk = jnp.exp(log_g * RT_CHUNK)[:, None, None]

    def chunks(t):
        return jnp.moveaxis(t.astype(jnp.float32).reshape(B, H, nc, RT_CHUNK, t.shape[-1]), 2, 0)

    def step(state, qkv):
        qc, kc, vc = qkv
        scores = jnp.einsum('bhqd,bhkd->bhqk', qc, kc) * decay_intra
        y = jnp.einsum('bhqk,bhke->bhqe', scores, vc) + jnp.einsum('bhqd,bhde->bhqe', qc, state) * decay_q
        state = state * decay_chunk + jnp.einsum('bhkd,bhke->bhde', kc * decay_k, vc)
        return state, y

    state, ys = lax.scan(step, state0, (chunks(q), chunks(k), chunks(v)))
    return jnp.moveaxis(ys, 0, 2).reshape(B, H, n, dv), state


def _final_state(k, v, log_g):
    n = k.shape[2]
    w = jnp.exp(log_g[:, None] * (n - 1.0 - jnp.arange(n, dtype=jnp.float32)))
    return jnp.einsum('bhnd,hn,bhne->bhde', k.astype(jnp.float32), w, v.astype(jnp.float32))


def _retention_direction(q, k, v, n_ctx, log_g, need_ctx):
    qc, kc, vc = q[:, :, :n_ctx], k[:, :, :n_ctx], v[:, :, :n_ctx]
    if need_ctx:
        zero = jnp.zeros((q.shape[0], RT_HEADS, RT_QK_DIM, RT_V_DIM), jnp.float32)
        y_ctx, s_ctx = _retention_scan(qc, kc, vc, log_g, zero)
    else:
        y_ctx, s_ctx = None, _final_state(kc, vc, log_g)
    y_lat, _ = _retention_scan(q[:, :, n_ctx:], k[:, :, n_ctx:], v[:, :, n_ctx:], log_g, s_ctx)
    return y_ctx, y_lat


def _flip_parts(t, n_ctx):
    return jnp.concatenate([jnp.flip(t[:, :, :n_ctx], axis=2), jnp.flip(t[:, :, n_ctx:], axis=2)], axis=2)


def _retention(h, n_ctx, w_in, decay, w_o, cos, sin, need_ctx):
    B, N, _ = h.shape
    q, k, v, g_f, g_b = jnp.split(h @ w_in, RT_SPLITS, axis=-1)

    def rope_lat(t):
        return jnp.concatenate([t[:, :n_ctx], _apply_rope(t[:, n_ctx:], cos, sin)], axis=1)

    q = jnp.transpose(rope_lat(q.reshape(B, N, RT_HEADS, RT_QK_DIM)), (0, 2, 1, 3))
    k = jnp.transpose(rope_lat(k.reshape(B, N, RT_HEADS, RT_QK_DIM)) * RT_SCALE, (0, 2, 1, 3))
    v = jnp.transpose(v.reshape(B, N, RT_HEADS, RT_V_DIM), (0, 2, 1, 3))
    log_g = jnp.log1p(-jnp.exp(decay.astype(jnp.float32)))
    yc_f, yl_f = _retention_direction(q, k, v, n_ctx, log_g[0], need_ctx)
    yc_b, yl_b = _retention_direction(_flip_parts(q, n_ctx), _flip_parts(k, n_ctx), _flip_parts(v, n_ctx),
                                      n_ctx, log_g[1], need_ctx)
    yl_b = jnp.flip(yl_b, axis=2)
    if need_ctx:
        y_f = jnp.concatenate([yc_f, yl_f], axis=2)
        y_b = jnp.concatenate([jnp.flip(yc_b, axis=2), yl_b], axis=2)
        gf, gb = g_f, g_b
    else:
        y_f, y_b = yl_f, yl_b
        gf, gb = g_f[:, n_ctx:], g_b[:, n_ctx:]

    def group_norm(y):
        y = jnp.transpose(y, (0, 2, 1, 3))
        return _rms_normalize(y).reshape(B, y.shape[1], RT_V)

    o = group_norm(y_f) * jax.nn.silu(gf.astype(jnp.float32)) + group_norm(y_b) * jax.nn.silu(gb.astype(jnp.float32))
    return o.astype(h.dtype) @ w_o


def _hier_moe(h, w_group, b_group, w_expert, b_expert, w_gate, w_up, w_down):
    B, N, D = h.shape
    t = h.reshape(B * N, D)
    p_group = jax.nn.softmax((t @ w_group + b_group).astype(jnp.float32), axis=-1)
    p_sel, g_idx = lax.top_k(p_group, 1)
    e_logits = (t @ w_expert + b_expert).astype(jnp.float32).reshape(B * N, MOE_GROUPS, MOE_EXPERTS_PER_GROUP)
    e_logits = jnp.einsum('tge,tg->te', e_logits, jax.nn.one_hot(g_idx[:, 0], MOE_GROUPS, dtype=jnp.float32))
    top_logit, e_idx = lax.top_k(e_logits, MOE_TOP_K)
    w_sel = jax.nn.softmax(top_logit, axis=-1) * p_sel
    expert_id = g_idx * MOE_EXPERTS_PER_GROUP + e_idx
    combine = jnp.einsum('tk,tke->te', w_sel,
                         jax.nn.one_hot(expert_id, MOE_EXPERTS, dtype=jnp.float32)).astype(h.dtype)
    y = jnp.zeros_like(t)
    for e in range(MOE_EXPERTS):
        a = jax.nn.silu(t @ w_gate[e]) * (t @ w_up[e])
        y = y + combine[:, e:e + 1] * (a @ w_down[e])
    return y.reshape(B, N, D)


def setup_inputs(seed: int = 0) -> dict:
    key = jax.random.key(seed)
    ks = jax.random.split(key, 24)
    D = D_MODEL
    n_da = (DEPTH + N_MIXERS - 1) // N_MIXERS
    n_rt = DEPTH // N_MIXERS

    def nrm(k, shape, s):
        return jax.random.normal(k, shape, jnp.float32) * s

    rt_decay = (-(5.0 + jnp.arange(RT_HEADS, dtype=jnp.float32)) * math.log(2.0)
                + nrm(ks[15], (n_rt, 2, RT_HEADS), 0.05))
    return {
        "x": nrm(ks[0], (BATCH, SEQ, D), 1.0),
        "c": nrm(ks[1], (BATCH, D), 1.0),
        "ctx": nrm(ks[2], (BATCH, CTX_LEN, D), 1.0),
        "c_ctx": nrm(ks[3], (D,), 1.0),
        "norm1_g": 1.0 + nrm(ks[4], (DEPTH, D), 0.02),
        "norm2_g": 1.0 + nrm(ks[5], (DEPTH, D), 0.02),
        "ada_w": nrm(ks[6], (DEPTH, D, 6 * D), 0.5 * D ** -0.5),
        "ada_b": nrm(ks[7], (DEPTH, 6 * D), 0.02),
        "da_w_qkv": nrm(ks[8], (n_da, D, 3 * DA_QK), D ** -0.5),
        "da_q_norm_g": 1.0 + nrm(ks[9], (n_da, DA_HEAD_DIM), 0.02),
        "da_k_norm_g": 1.0 + nrm(ks[10], (n_da, DA_HEAD_DIM), 0.02),
        "da_lambda": nrm(ks[11], (n_da, 4, DA_HEAD_DIM), 0.1),
        "da_subln_g": 1.0 + nrm(ks[12], (n_da, DA_V_DIM), 0.02),
        "da_w_o": nrm(ks[13], (n_da, DA_HEADS * DA_V_DIM, D), (DA_HEADS * DA_V_DIM) ** -0.5),
        "rt_w_in": nrm(ks[14], (n_rt, D, RT_IN), D ** -0.5),
        "rt_decay": rt_decay,
        "rt_w_o": nrm(ks[16], (n_rt, RT_V, D), RT_V ** -0.5),
        "moe_w_group": nrm(ks[17], (DEPTH, D, MOE_GROUPS), D ** -0.5),
        "moe_b_group": nrm(ks[18], (DEPTH, MOE_GROUPS), 0.01),
        "moe_w_expert": nrm(ks[19], (DEPTH, D, MOE_EXPERTS), D ** -0.5),
        "moe_b_expert": nrm(ks[20], (DEPTH, MOE_EXPERTS), 0.01),
        "moe_w_gate": nrm(ks[21], (DEPTH, MOE_EXPERTS, D, MOE_HIDDEN), D ** -0.5),
        "moe_w_up": nrm(ks[22], (DEPTH, MOE_EXPERTS, D, MOE_HIDDEN), D ** -0.5),
        "moe_w_down": nrm(ks[23], (DEPTH, MOE_EXPERTS, MOE_HIDDEN, D), MOE_HIDDEN ** -0.5),
    }


def reference(x, c, ctx, c_ctx, norm1_g, norm2_g, ada_w, ada_b, da_w_qkv, da_q_norm_g, da_k_norm_g,
              da_lambda, da_subln_g, da_w_o, rt_w_in, rt_decay, rt_w_o, moe_w_group, moe_b_group,
              moe_w_expert, moe_b_expert, moe_w_gate, moe_w_up, moe_w_down):
    S = x.shape[1]
    n_ctx = ctx.shape[1]
    cos_da, sin_da = _axial_rope(S, DA_HEAD_DIM)
    cos_rt, sin_rt = _axial_rope(S, RT_QK_DIM)
    silu_c = jax.nn.silu(c)[:, None, :]
    silu_cc = jax.nn.silu(c_ctx)[None, None, :]
    h_lat, h_ctx = x, ctx
    for i in range(DEPTH):
        need_ctx = i < DEPTH - 1
        m_lat = jnp.split(silu_c @ ada_w[i] + ada_b[i], 6, axis=-1)
        m_ctx = jnp.split(silu_cc @ ada_w[i] + ada_b[i], 6, axis=-1)

        a = jnp.concatenate([_modulate(_rms_norm(h_ctx, norm1_g[i]), m_ctx[0], m_ctx[1]),
                             _modulate(_rms_norm(h_lat, norm1_g[i]), m_lat[0], m_lat[1])], axis=1)
        j = i // N_MIXERS
        if i % N_MIXERS == 0:
            lam_init = 0.8 - 0.6 * math.exp(-0.3 * i)
            o = _diff_attention(a, n_ctx, da_w_qkv[j], da_q_norm_g[j], da_k_norm_g[j], da_lambda[j],
                                da_subln_g[j], da_w_o[j], lam_init, cos_da, sin_da, need_ctx)
        else:
            o = _retention(a, n_ctx, rt_w_in[j], rt_decay[j], rt_w_o[j], cos_rt, sin_rt, need_ctx)
        h_lat = h_lat + m_lat[2] * o[:, -S:]

        moe = (moe_w_group[i], moe_b_group[i], moe_w_expert[i], moe_b_expert[i],
               moe_w_gate[i], moe_w_up[i], moe_w_down[i])
        f_lat_in = _modulate(_rms_norm(h_lat, norm2_g[i]), m_lat[3], m_lat[4])
        if need_ctx:
            h_ctx = h_ctx + m_ctx[2] * o[:, :n_ctx]
            f_ctx_in = _modulate(_rms_norm(h_ctx, norm2_g[i]), m_ctx[3], m_ctx[4])
            f = _hier_moe(jnp.concatenate([f_ctx_in, f_lat_in], axis=1), *moe)
            h_ctx = h_ctx + m_ctx[5] * f[:, :n_ctx]
            h_lat = h_lat + m_lat[5] * f[:, n_ctx:]
        else:
            f = _hier_moe(f_lat_in, *moe)
            h_lat = h_lat + m_lat[5] * f
    return h_lat
```

```python
import functools
import math

import jax
import jax.numpy as jnp
import numpy as np
from jax import lax
from jax.experimental import pallas as pl
from jax.experimental.pallas import tpu as pltpu

F32 = jnp.float32
BF16 = jnp.bfloat16

LANES = 128
SUBLANES = 8
VMEM_LIMIT_BYTES = 56 * 1024 * 1024

ROW_TILE = 256
RMS_EPS = 1e-6
ROPE_BASE = 10000.0
GRID_W = 64

DA_HEADS = 8
DA_HEAD_DIM = 64
DA_SCALE = DA_HEAD_DIM ** -0.5
RT_HEADS = 4
RT_QK_DIM = 256
RT_V_DIM = 512
RT_SCALE = RT_QK_DIM ** -0.5
RT_CHUNK = 128
MOE_GROUPS = 4
MOE_PER_GROUP = 8
MOE_EXPERTS = MOE_GROUPS * MOE_PER_GROUP
MOE_PAIRS = MOE_PER_GROUP * (MOE_PER_GROUP - 1) // 2
MOE_CLASSES = MOE_GROUPS * MOE_PAIRS
MOE_TILE = 256
META_W = LANES

NEG_INF = float("-inf")


def _dot(a, b):
    return jnp.dot(a, b, preferred_element_type=F32)


def _dot_nt(a, b):
    return lax.dot_general(a, b, (((1,), (1,)), ((), ())), preferred_element_type=F32)


def _dot_tn(a, b):
    return lax.dot_general(a, b, (((0,), (0,)), ((), ())), preferred_element_type=F32)


def _split_bf16(x):
    hi = x.astype(BF16)
    lo = (x - hi.astype(F32)).astype(BF16)
    return hi, lo


def _params(semantics):
    return pltpu.CompilerParams(dimension_semantics=semantics, vmem_limit_bytes=VMEM_LIMIT_BYTES)


def _ada_kernel(c_ref, w_ref, b_ref, o_ref):
    c = c_ref[...]
    s_hi, s_lo = _split_bf16(c * jax.nn.sigmoid(c))
    w_hi, w_lo = _split_bf16(w_ref[0])
    o_ref[0] = _dot(s_hi, w_hi) + _dot(s_lo, w_hi) + _dot(s_hi, w_lo) + b_ref[0]


def _ada_table(cc, ada_w, ada_b):
    depth, d, six_d = ada_w.shape
    rows = cc.shape[0]
    tn = six_d // 4
    return pl.pallas_call(
        _ada_kernel,
        grid=(depth, six_d // tn),
        in_specs=[
            pl.BlockSpec((rows, d), lambda i, j: (0, 0)),
            pl.BlockSpec((1, d, tn), lambda i, j: (i, 0, j)),
            pl.BlockSpec((1, 1, tn), lambda i, j: (i, 0, j)),
        ],
        out_specs=pl.BlockSpec((1, rows, tn), lambda i, j: (i, 0, j)),
        out_shape=jax.ShapeDtypeStruct((depth, rows, six_d), F32),
        compiler_params=_params(("parallel", "parallel")),
        name="ada_table",
    )(cc, ada_w, ada_b.reshape(depth, 1, six_d))


def _norm_modulate(hn, g, shift, scale):
    ms = jnp.mean(hn * hn, axis=-1, keepdims=True)
    return (hn * lax.rsqrt(ms + RMS_EPS) * g) * (1.0 + scale) + shift


def _swap_half_segments(x, seg):
    lane = lax.broadcasted_iota(jnp.int32, x.shape, 1)
    first = (lane % seg) < (seg // 2)
    return jnp.where(first, pltpu.roll(x, LANES - seg // 2, 1), pltpu.roll(x, seg // 2, 1))


def _proj_kernel(*refs, mixer, has_f):
    if has_f:
        h_ref, f_ref, pmod_ref, *refs = refs
    else:
        h_ref, *refs = refs
    if mixer == "da":
        mod_ref, g_ref, w_ref, cos_ref, sin_ref, qg_ref, kg_ref, seg_ref, hout_ref, o_ref = refs
    else:
        mod_ref, g_ref, w_ref, cos_ref, sin_ref, hout_ref, o_ref = refs

    hn = h_ref[0]
    if has_f:
        hn = hn + pmod_ref[0, 0, 5:6, :] * f_ref[...]
        hout_ref[0] = hn
    mod = mod_ref[0, 0]
    a = _norm_modulate(hn, g_ref[...], mod[0:1], mod[1:2]).astype(BF16)
    cos = cos_ref[...]
    sin = sin_ref[...]
    n_out = w_ref.shape[1]
    d_model = w_ref.shape[0]
    chunk = 1024
    for cb in range(n_out // chunk):
        y = _dot(a, w_ref[:, cb * chunk:(cb + 1) * chunk])
        if mixer == "da":
            for hb in range(chunk // LANES):
                col = cb * chunk + hb * LANES
                yb = y[:, hb * LANES:(hb + 1) * LANES]
                if col < 2 * d_model:
                    y2_hi, y2_lo = _split_bf16(yb * yb)
                    ms = _dot(y2_hi, seg_ref[...]) + _dot(y2_lo, seg_ref[...])
                    gain = qg_ref[...] if col < d_model else kg_ref[...]
                    yn = yb * lax.rsqrt(ms + RMS_EPS) * gain
                    yb = yn * cos + _swap_half_segments(yn, DA_HEAD_DIM) * sin
                    if col < d_model:
                        yb = yb * DA_SCALE
                o_ref[0, :, col:col + LANES] = yb.astype(BF16)
        else:
            for hb in range(chunk // RT_QK_DIM):
                col = cb * chunk + hb * RT_QK_DIM
                yb = y[:, hb * RT_QK_DIM:(hb + 1) * RT_QK_DIM]
                if col < 2 * RT_HEADS * RT_QK_DIM:
                    x1 = yb[:, :LANES]
                    x2 = yb[:, LANES:]
                    r1 = x1 * cos - x2 * sin
                    r2 = x1 * sin + x2 * cos
                    if col >= RT_HEADS * RT_QK_DIM:
                        r1 = r1 * RT_SCALE
                        r2 = r2 * RT_SCALE
                    o_ref[0, :, col:col + LANES] = r1.astype(BF16)
                    o_ref[0, :, col + LANES:col + 2 * LANES] = r2.astype(BF16)
                else:
                    o_ref[0, :, col:col + RT_QK_DIM] = yb.astype(BF16)


def _norm_proj(h, f, pmod, mod, g, w, cos, sin, extra, *, mixer, n_lat_tiles):
    b, n, d = h.shape
    n_out = w.shape[1]
    tiles = n // ROW_TILE
    has_f = f is not None
    row = lambda i, t: (i, t, 0)
    mod_spec = pl.BlockSpec((1, 1, SUBLANES, d), lambda i, t: (i, t // n_lat_tiles, 0, 0))
    const = lambda shape: pl.BlockSpec(shape, lambda i, t: (0,) * len(shape))
    in_specs = [pl.BlockSpec((1, ROW_TILE, d), row)]
    args = [h]
    if has_f:
        in_specs += [pl.BlockSpec((ROW_TILE, d), lambda i, t: (i * tiles + t, 0)), mod_spec]
        args += [f, pmod]
    in_specs += [
        mod_spec,
        const((1, d)),
        pl.BlockSpec((d, n_out), lambda i, t: (0, 0), pipeline_mode=pl.Buffered(1)),
        pl.BlockSpec((ROW_TILE, LANES), lambda i, t: (t, 0)),
        pl.BlockSpec((ROW_TILE, LANES), lambda i, t: (t, 0)),
    ]
    args += [mod, g.reshape(1, d), w, cos, sin]
    for e in extra:
        in_specs.append(const(e.shape))
        args.append(e)
    out_shape = [jax.ShapeDtypeStruct((b, n, d), F32), jax.ShapeDtypeStruct((b, n, n_out), BF16)]
    out_specs = [pl.BlockSpec((1, ROW_TILE, d), row), pl.BlockSpec((1, ROW_TILE, n_out), row)]
    if not has_f:
        out_shape, out_specs = out_shape[1:], out_specs[1:]
        kernel = functools.partial(_proj_kernel_no_residual, mixer=mixer)
    else:
        kernel = functools.partial(_proj_kernel, mixer=mixer, has_f=True)
    res = pl.pallas_call(
        kernel,
        grid=(b, tiles),
        in_specs=in_specs,
        out_specs=out_specs,
        out_shape=out_shape,
        compiler_params=_params(("parallel", "parallel")),
        name=f"norm_proj_{mixer}",
    )(*args)
    if has_f:
        return res[0], res[1]
    return h, res[0]


def _proj_kernel_no_residual(*refs, mixer):
    *ins, o_ref = refs
    _proj_kernel(*ins, None, o_ref, mixer=mixer, has_f=False)


def _diff_attn_kernel(q_ref, k_ref, v_ref, lam_ref, sg_ref, o_ref, *, n_lat, lam_init):
    t = pl.program_id(2)
    lane = lax.broadcasted_iota(jnp.int32, (1, LANES), 1)
    first = lane < DA_HEAD_DIM
    q = q_ref[0]
    zero = jnp.zeros_like(q)
    q1 = jnp.where(first, q, zero)
    q2 = jnp.where(first, zero, q)
    lam = lam_ref[...]
    lam_full = (jnp.exp(jnp.sum(lam[0:1] * lam[1:2], axis=-1, keepdims=True))
                - jnp.exp(jnp.sum(lam[2:3] * lam[3:4], axis=-1, keepdims=True)) + lam_init)

    def attend(k, v):
        def one_map(qm):
            s = _dot_nt(qm, k)
            p = jnp.exp(s - jnp.max(s, axis=-1, keepdims=True))
            return _dot(p.astype(BF16), v) / jnp.sum(p, axis=-1, keepdims=True)
        a = one_map(q1) - lam_full * one_map(q2)
        ms = jnp.mean(a * a, axis=-1, keepdims=True)
        o_ref[0] = (a * lax.rsqrt(ms + RMS_EPS) * sg_ref[...] * (1.0 - lam_init)).astype(BF16)

    @pl.when(t < n_lat // ROW_TILE)
    def _():
        attend(k_ref[0], v_ref[0])

    @pl.when(t >= n_lat // ROW_TILE)
    def _():
        attend(k_ref[0, n_lat:, :], v_ref[0, n_lat:, :])


def _diff_attention(qkv, lam, subln_g, *, n_lat, lam_init):
    b, n, _ = qkv.shape
    hd = 2 * DA_HEAD_DIM
    return pl.pallas_call(
        functools.partial(_diff_attn_kernel, n_lat=n_lat, lam_init=lam_init),
        grid=(b, DA_HEADS, n // ROW_TILE),
        in_specs=[
            pl.BlockSpec((1, ROW_TILE, hd), lambda i, h, t: (i, t, h)),
            pl.BlockSpec((1, n, hd), lambda i, h, t: (i, 0, DA_HEADS + h)),
            pl.BlockSpec((1, n, hd), lambda i, h, t: (i, 0, 2 * DA_HEADS + h)),
            pl.BlockSpec(lam.shape, lambda i, h, t: (0, 0)),
            pl.BlockSpec((1, hd), lambda i, h, t: (0, 0)),
        ],
        out_specs=pl.BlockSpec((1, ROW_TILE, hd), lambda i, h, t: (i, t, h)),
        out_shape=jax.ShapeDtypeStruct((b, n, DA_HEADS * hd), BF16),
        compiler_params=_params(("parallel", "parallel", "arbitrary")),
        name="diff_attention",
    )(qkv, qkv, qkv, lam, subln_g.reshape(1, hd))


def _retention_kernel(lg_ref, q_ref, k_ref, v_ref, gf_ref, gb_ref, o_ref, sf_ref, sb_ref, yf_ref, yb_ref,
                      *, n_lat):
    head = pl.program_id(1)
    n = q_ref.shape[1]
    c = RT_CHUNK
    n_chunks = n // c
    lat_chunks = n_lat // c
    lg_f = jnp.full((1, 1), lg_ref[0, head], F32)
    lg_b = jnp.full((1, 1), lg_ref[1, head], F32)
    pos_r = lax.broadcasted_iota(jnp.int32, (c, c), 0).astype(F32)
    pos_c = lax.broadcasted_iota(jnp.int32, (c, c), 1).astype(F32)
    rel = pos_r - pos_c
    col = lax.broadcasted_iota(jnp.int32, (c, 1), 0).astype(F32)
    intra_f = jnp.where(rel >= 0, jnp.exp(lg_f * jnp.maximum(rel, 0.0)), 0.0)
    intra_b = jnp.where(rel <= 0, jnp.exp(lg_b * jnp.maximum(-rel, 0.0)), 0.0)
    dq_f = jnp.exp(lg_f * (col + 1.0))
    dk_f = jnp.exp(lg_f * (c - 1.0 - col))
    dq_b = jnp.exp(lg_b * (c - col))
    dk_b = jnp.exp(lg_b * col)
    dc_f = jnp.exp(lg_f * c)
    dc_b = jnp.exp(lg_b * c)
    sf_ref[...] = jnp.zeros_like(sf_ref)
    sb_ref[...] = jnp.zeros_like(sb_ref)

    def one_chunk(idx, s_ref, intra, dq, dk, dc, g_ref, y_ref):
        rows = pl.ds(pl.multiple_of(idx * c, c), c)
        qc = q_ref[0, rows, :]
        kc = k_ref[0, rows, :]
        vc = v_ref[0, rows, :]
        state = s_ref[...]
        scores = (_dot_nt(qc, kc) * intra).astype(BF16)
        y = _dot(scores, vc) + _dot(qc, state.astype(BF16)) * dq
        s_ref[...] = state * dc + _dot_tn((kc.astype(F32) * dk).astype(BF16), vc)
        ms = jnp.mean(y * y, axis=-1, keepdims=True)
        gate = g_ref[0, rows, :].astype(F32)
        y_ref[rows, :] = y * lax.rsqrt(ms + RMS_EPS) * (gate * jax.nn.sigmoid(gate))

    def step(s, carry):
        idx_f = (s + lat_chunks) % n_chunks
        idx_b = n_chunks - 1 - s
        one_chunk(idx_f, sf_ref, intra_f, dq_f, dk_f, dc_f, gf_ref, yf_ref)
        one_chunk(idx_b, sb_ref, intra_b, dq_b, dk_b, dc_b, gb_ref, yb_ref)
        return carry

    lax.fori_loop(0, n_chunks, step, 0)
    o_ref[0] = (yf_ref[...] + yb_ref[...]).astype(BF16)


def _retention(proj, log_g, *, n_lat):
    b, n, _ = proj.shape
    kq = RT_HEADS * RT_QK_DIM // RT_QK_DIM
    v0 = 2 * RT_HEADS * RT_QK_DIM // RT_V_DIM
    return pl.pallas_call(
        functools.partial(_retention_kernel, n_lat=n_lat),
        grid=(b, RT_HEADS),
        in_specs=[
            pl.BlockSpec(memory_space=pltpu.SMEM),
            pl.BlockSpec((1, n, RT_QK_DIM), lambda i, h: (i, 0, h)),
            pl.BlockSpec((1, n, RT_QK_DIM), lambda i, h: (i, 0, kq + h)),
            pl.BlockSpec((1, n, RT_V_DIM), lambda i, h: (i, 0, v0 + h)),
            pl.BlockSpec((1, n, RT_V_DIM), lambda i, h: (i, 0, v0 + RT_HEADS + h)),
            pl.BlockSpec((1, n, RT_V_DIM), lambda i, h: (i, 0, v0 + 2 * RT_HEADS + h)),
        ],
        out_specs=pl.BlockSpec((1, n, RT_V_DIM), lambda i, h: (i, 0, h)),
        out_shape=jax.ShapeDtypeStruct((b, n, RT_HEADS * RT_V_DIM), BF16),
        scratch_shapes=[
            pltpu.VMEM((RT_QK_DIM, RT_V_DIM), F32),
            pltpu.VMEM((RT_QK_DIM, RT_V_DIM), F32),
            pltpu.VMEM((n, RT_V_DIM), F32),
            pltpu.VMEM((n, RT_V_DIM), F32),
        ],
        compiler_params=_params(("parallel", "arbitrary")),
        name="retention",
    )(log_g, proj, proj, proj, proj, proj)


def _route(logits):
    lane = lax.broadcasted_iota(jnp.int32, logits.shape, 1).astype(F32)
    big = float(LANES)

    def top(vals):
        m = jnp.max(vals, axis=-1, keepdims=True)
        idx = jnp.min(jnp.where(vals == m, lane, big), axis=-1, keepdims=True)
        return m, idx

    gl = jnp.where(lane < MOE_GROUPS, logits, NEG_INF)
    gmax, gidx = top(gl)
    p_sel = 1.0 / jnp.sum(jnp.exp(gl - gmax), axis=-1, keepdims=True)
    lo_lane = MOE_GROUPS + MOE_PER_GROUP * gidx
    el = jnp.where((lane >= lo_lane) & (lane < lo_lane + MOE_PER_GROUP), logits, NEG_INF)
    m1, i1 = top(el)
    m2, i2 = top(jnp.where(lane == i1, NEG_INF, el))
    e2 = jnp.exp(m2 - m1)
    w1 = p_sel / (1.0 + e2)
    w2 = p_sel * e2 / (1.0 + e2)
    first_lo = i1 < i2
    e_lo = jnp.where(first_lo, i1, i2) - MOE_GROUPS
    e_hi = jnp.where(first_lo, i2, i1) - MOE_GROUPS
    w_lo = jnp.where(first_lo, w1, w2)
    w_hi = jnp.where(first_lo, w2, w1)
    a = e_lo - MOE_PER_GROUP * gidx
    bb = e_hi - MOE_PER_GROUP * gidx
    pair = a * MOE_PER_GROUP - a * (a + 1.0) * 0.5 + (bb - a - 1.0)
    cls = gidx * MOE_PAIRS + pair
    out = jnp.zeros_like(logits)
    for k, val in enumerate((cls, w_lo, w_hi, e_lo, e_hi)):
        out = jnp.where(lane == k, val, out)
    return out


def _out_proj_kernel(o_ref, h_ref, mod_ref, g_ref, wo_ref, wr_hi_ref, wr_lo_ref, br_ref, hmid_ref, fin_ref):
    d = h_ref.shape[2]
    mod = mod_ref[0, 0]
    hn = h_ref[0] + mod[2:3] * _dot(o_ref[0], wo_ref[...])
    hmid_ref[0] = hn
    fin = _norm_modulate(hn, g_ref[...], mod[3:4], mod[4:5])
    f_hi, f_lo = _split_bf16(fin)
    logits = (_dot(f_hi, wr_hi_ref[...]) + _dot(f_lo, wr_hi_ref[...]) + _dot(f_hi, wr_lo_ref[...])
              + br_ref[...])
    fin_ref[:, :d] = fin
    fin_ref[:, d:] = _route(logits)


def _out_proj(o, h, mod, g, wo, wr_hi, wr_lo, br, *, n_lat_tiles):
    b, n, d = h.shape
    dv = o.shape[2]
    tiles = n // ROW_TILE
    row = lambda i, t: (i, t, 0)
    const = lambda shape: pl.BlockSpec(shape, lambda i, t: (0,) * len(shape))
    return pl.pallas_call(
        _out_proj_kernel,
        grid=(b, tiles),
        in_specs=[
            pl.BlockSpec((1, ROW_TILE, dv), row),
            pl.BlockSpec((1, ROW_TILE, d), row),
            pl.BlockSpec((1, 1, SUBLANES, d), lambda i, t: (i, t // n_lat_tiles, 0, 0)),
            const((1, d)),
            const((dv, d)),
            const((d, LANES)),
            const((d, LANES)),
            const((1, LANES)),
        ],
        out_specs=[
            pl.BlockSpec((1, ROW_TILE, d), row),
            pl.BlockSpec((ROW_TILE, d + META_W), lambda i, t: (i * tiles + t, 0)),
        ],
        out_shape=[
            jax.ShapeDtypeStruct((b, n, d), F32),
            jax.ShapeDtypeStruct((b * n, d + META_W), F32),
        ],
        compiler_params=_params(("parallel", "parallel")),
        name="out_proj_router",
    )(o, h, mod, g.reshape(1, d), wo, wr_hi, wr_lo, br)


def _moe_kernel(tlo_ref, thi_ref, cnt_ref, tok_ref, fin_hbm, wg_lo, wu_lo, wd_lo, wg_hi, wu_hi, wd_hi,
                out_hbm, xbuf, obuf, gsem, ssem):
    i = pl.program_id(0)
    d = obuf.shape[1]
    count = cnt_ref[i]

    def gather(r):
        return pltpu.make_async_copy(fin_hbm.at[pl.ds(tok_ref[0, 0, r], 1), :], xbuf.at[pl.ds(r, 1), :], gsem)

    def scatter(r):
        return pltpu.make_async_copy(obuf.at[pl.ds(r, 1), :], out_hbm.at[pl.ds(tok_ref[0, 0, r], 1), :], ssem)

    def for_rows(fn):
        def body(r, carry):
            fn(r)
            return carry
        lax.fori_loop(0, count, body, 0)

    @pl.when(i == 0)
    def _():
        xbuf[...] = jnp.zeros_like(xbuf)

    @pl.when(count > 0)
    def _():
        for_rows(lambda r: gather(r).start())
        for_rows(lambda r: gather(r).wait())
        x = xbuf[:, :d].astype(BF16)
        meta = xbuf[:, d:]
        acc = None
        for k, (wg, wu, wd) in enumerate(((wg_lo, wu_lo, wd_lo), (wg_hi, wu_hi, wd_hi))):
            gate = _dot(x, wg[0])
            a = (gate * jax.nn.sigmoid(gate)) * _dot(x, wu[0])
            y = meta[:, 1 + k:2 + k] * _dot(a.astype(BF16), wd[0])
            acc = y if acc is None else acc + y
        obuf[...] = acc
        for_rows(lambda r: scatter(r).start())
        for_rows(lambda r: scatter(r).wait())


def _moe_plan(cls, n_tokens):
    tm = MOE_TILE
    n_tiles_max = n_tokens // tm + MOE_CLASSES
    lo_tab, hi_tab = [], []
    for g in range(MOE_GROUPS):
        for a in range(MOE_PER_GROUP):
            for bb in range(a + 1, MOE_PER_GROUP):
                lo_tab.append(g * MOE_PER_GROUP + a)
                hi_tab.append(g * MOE_PER_GROUP + bb)
    lo_tab = jnp.asarray(np.array(lo_tab, np.int32))
    hi_tab = jnp.asarray(np.array(hi_tab, np.int32))
    order = jnp.argsort(cls, stable=True).astype(jnp.int32)
    bounds = jnp.searchsorted(cls[order], jnp.arange(MOE_CLASSES + 1, dtype=jnp.int32), side="left")
    starts = bounds[:-1].astype(jnp.int32)
    counts = (bounds[1:] - bounds[:-1]).astype(jnp.int32)
    tiles_per = (counts + tm - 1) // tm
    tile_cum = jnp.cumsum(tiles_per)
    n_tiles = tile_cum[-1]
    tile_ids = jnp.arange(n_tiles_max, dtype=jnp.int32)
    tile_cls = jnp.searchsorted(tile_cum, jnp.minimum(tile_ids, n_tiles - 1), side="right").astype(jnp.int32)
    tile_cls = jnp.minimum(tile_cls, MOE_CLASSES - 1)
    slot = jnp.arange(n_tiles_max * tm, dtype=jnp.int32)
    s_cls = tile_cls[slot // tm]
    j = slot - tm * (tile_cum[s_cls] - tiles_per[s_cls])
    valid = (j < counts[s_cls]) & (slot // tm < n_tiles)
    tok = jnp.where(valid, order[jnp.clip(starts[s_cls] + j, 0, n_tokens - 1)], 0)
    tile_cnt = jnp.sum(valid.reshape(n_tiles_max, tm).astype(jnp.int32), axis=1)
    return lo_tab[tile_cls], hi_tab[tile_cls], tile_cnt, tok.reshape(n_tiles_max, 1, tm)


def _moe(fin, w_gate, w_up, w_down):
    n_tokens, width = fin.shape
    d = width - META_W
    hidden = w_gate.shape[2]
    tm = MOE_TILE
    cls = fin[:, d].astype(jnp.int32)
    tile_lo, tile_hi, tile_cnt, tok = _moe_plan(cls, n_tokens)
    n_tiles_max = tok.shape[0]
    idx_spec = pl.BlockSpec((1, 1, tm), lambda i, lo, hi, nt: (i, 0, 0), memory_space=pltpu.SMEM)
    w_in = lambda sel: pl.BlockSpec((1, d, hidden), lambda i, lo, hi, nt: ((lo, hi)[sel][i], 0, 0))
    w_out = lambda sel: pl.BlockSpec((1, hidden, d), lambda i, lo, hi, nt: ((lo, hi)[sel][i], 0, 0))
    return pl.pallas_call(
        _moe_kernel,
        grid_spec=pltpu.PrefetchScalarGridSpec(
            num_scalar_prefetch=3,
            grid=(n_tiles_max,),
            in_specs=[
                idx_spec,
                pl.BlockSpec(memory_space=pl.ANY),
                w_in(0), w_in(0), w_out(0), w_in(1), w_in(1), w_out(1),
            ],
            out_specs=pl.BlockSpec(memory_space=pl.ANY),
            scratch_shapes=[
                pltpu.VMEM((tm, width), F32),
                pltpu.VMEM((tm, d), F32),
                pltpu.SemaphoreType.DMA(()),
                pltpu.SemaphoreType.DMA(()),
            ],
        ),
        out_shape=jax.ShapeDtypeStruct((n_tokens, d), F32),
        compiler_params=_params(("arbitrary",)),
        name="moe_experts",
    )(tile_lo, tile_hi, tile_cnt, tok, fin, w_gate, w_up, w_down, w_gate, w_up, w_down)


def _final_kernel(h_ref, f_ref, mod_ref, o_ref):
    o_ref[0] = h_ref[0] + mod_ref[0, 0, 5:6, :] * f_ref[...]


def _final_residual(h, f, mod, *, n_lat):
    b, n, d = h.shape
    tiles = n // ROW_TILE
    lat_tiles = n_lat // ROW_TILE
    return pl.pallas_call(
        _final_kernel,
        grid=(b, lat_tiles),
        in_specs=[
            pl.BlockSpec((1, ROW_TILE, d), lambda i, t: (i, t, 0)),
            pl.BlockSpec((ROW_TILE, d), lambda i, t: (i * tiles + t, 0)),
            pl.BlockSpec((1, 1, SUBLANES, d), lambda i, t: (i, 0, 0, 0)),
        ],
        out_specs=pl.BlockSpec((1, ROW_TILE, d), lambda i, t: (i, t, 0)),
        out_shape=jax.ShapeDtypeStruct((b, n_lat, d), F32),
        compiler_params=_params(("parallel", "parallel")),
        name="final_residual",
    )(h, f, mod)


def _rope_tables(n_lat, n_ctx, head_dim, sign_pattern):
    t = np.arange(n_lat)
    row = (t // GRID_W).astype(np.float32)
    colp = (t % GRID_W).astype(np.float32)
    n_freq = head_dim // 4
    inv_freq = jnp.asarray(ROPE_BASE, F32) ** (-jnp.arange(n_freq, dtype=F32) / n_freq)
    ang = jnp.concatenate([jnp.asarray(row)[:, None] * inv_freq, jnp.asarray(colp)[:, None] * inv_freq], axis=-1)
    cos, sin = jnp.cos(ang), jnp.sin(ang)
    reps = LANES // cos.shape[1]
    cos_t = jnp.tile(cos, (1, reps))
    sin_t = jnp.concatenate([sin * s for s in sign_pattern[:reps]], axis=1) if reps > 1 else sin
    cos_t = jnp.concatenate([cos_t, jnp.ones((n_ctx, LANES), F32)], axis=0)
    sin_t = jnp.concatenate([sin_t, jnp.zeros((n_ctx, LANES), F32)], axis=0)
    return cos_t, sin_t


def kernel(x, c, ctx, c_ctx, norm1_g, norm2_g, ada_w, ada_b, da_w_qkv, da_q_norm_g, da_k_norm_g, da_lambda,
           da_subln_g, da_w_o, rt_w_in, rt_decay, rt_w_o, moe_w_group, moe_b_group, moe_w_expert, moe_b_expert,
           moe_w_gate, moe_w_up, moe_w_down):
    b, n_lat, d = x.shape
    n_ctx = ctx.shape[1]
    depth = ada_w.shape[0]
    assert n_lat % ROW_TILE == 0 and n_ctx == ROW_TILE and d % LANES == 0
    n_lat_tiles = n_lat // ROW_TILE
    n_tokens = b * (n_lat + n_ctx)
    assert n_tokens % MOE_TILE == 0

    rows = -(-(b + 1) // SUBLANES) * SUBLANES
    cc = jnp.concatenate([c, c_ctx[None, :], jnp.zeros((rows - b - 1, d), F32)], axis=0)
    mods = _ada_table(cc, ada_w, ada_b).reshape(depth, rows, 6, d)
    mod_lat = mods[:, :b]
    mod_ctx = jnp.broadcast_to(mods[:, b][:, None], mod_lat.shape)
    modtab = jnp.stack([mod_lat, mod_ctx], axis=2)
    modtab = jnp.pad(modtab, ((0, 0), (0, 0), (0, 0), (0, SUBLANES - 6), (0, 0)))

    cos_da, sin_da = _rope_tables(n_lat, n_ctx, DA_HEAD_DIM, (-1.0, 1.0, -1.0, 1.0))
    cos_rt, sin_rt = _rope_tables(n_lat, n_ctx, RT_QK_DIM, (1.0,))
    seg = np.zeros((LANES, LANES), np.float32)
    for s in range(LANES // DA_HEAD_DIM):
        seg[s * DA_HEAD_DIM:(s + 1) * DA_HEAD_DIM, s * DA_HEAD_DIM:(s + 1) * DA_HEAD_DIM] = 1.0 / DA_HEAD_DIM
    seg = jnp.asarray(seg, BF16)

    h = jnp.concatenate([x, ctx], axis=1)
    f = None
    for i in range(depth):
        j = i // 2
        pmod = modtab[i - 1] if i > 0 else None
        if i % 2 == 0:
            lam_init = 0.8 - 0.6 * math.exp(-0.3 * i)
            reps = LANES // DA_HEAD_DIM
            extra = (jnp.tile(da_q_norm_g[j], reps)[None, :], jnp.tile(da_k_norm_g[j], reps)[None, :], seg)
            h, qkv = _norm_proj(h, f, pmod, modtab[i], norm1_g[i], da_w_qkv[j].astype(BF16), cos_da, sin_da,
                                extra, mixer="da", n_lat_tiles=n_lat_tiles)
            o = _diff_attention(qkv, da_lambda[j], da_subln_g[j], n_lat=n_lat, lam_init=lam_init)
            w_o = da_w_o[j]
        else:
            h, proj = _norm_proj(h, f, pmod, modtab[i], norm1_g[i], rt_w_in[j].astype(BF16), cos_rt, sin_rt,
                                 (), mixer="rt", n_lat_tiles=n_lat_tiles)
            log_g = jnp.log1p(-jnp.exp(rt_decay[j].astype(F32)))
            o = _retention(proj, log_g, n_lat=n_lat)
            w_o = rt_w_o[j]
        w_r = jnp.concatenate([moe_w_group[i], moe_w_expert[i]], axis=1)
        w_r = jnp.pad(w_r, ((0, 0), (0, LANES - w_r.shape[1])))
        b_r = jnp.pad(jnp.concatenate([moe_b_group[i], moe_b_expert[i]]), (0, LANES - MOE_GROUPS - MOE_EXPERTS))
        wr_hi, wr_lo = _split_bf16(w_r)
        h, fin = _out_proj(o, h, modtab[i], norm2_g[i], w_o.astype(BF16), wr_hi, wr_lo, b_r[None, :],
                           n_lat_tiles=n_lat_tiles)
        f = _moe(fin, moe_w_gate[i].astype(BF16), moe_w_up[i].astype(BF16), moe_w_down[i].astype(BF16))
    return _final_residual(h, f, modtab[depth - 1], n_lat=n_lat)
```

```python
import functools
import math

import jax
import jax.numpy as jnp
import numpy as np
from jax import lax
from jax.experimental import pallas as pl
from jax.experimental.pallas import tpu as pltpu

F32 = jnp.float32
BF16 = jnp.bfloat16

LANES = 128
SUBLANES = 8
VMEM_LIMIT_BYTES = 56 * 1024 * 1024

ROW_TILE = 256
RMS_EPS = 1e-6
ROPE_BASE = 10000.0
GRID_W = 64

DA_HEADS = 8
DA_HEAD_DIM = 64
DA_SCALE = DA_HEAD_DIM ** -0.5
RT_HEADS = 4
RT_QK_DIM = 256
RT_V_DIM = 512
RT_SCALE = RT_QK_DIM ** -0.5
RT_CHUNK = 128
MOE_GROUPS = 4
MOE_PER_GROUP = 8
MOE_EXPERTS = MOE_GROUPS * MOE_PER_GROUP
MOE_PAIRS = MOE_PER_GROUP * (MOE_PER_GROUP - 1) // 2
MOE_CLASSES = MOE_GROUPS * MOE_PAIRS
MOE_TILE = 256
META_W = LANES

NEG_INF = float("-inf")


def _dot(a, b):
    return jnp.dot(a, b, preferred_element_type=F32)


def _dot_nt(a, b):
    return lax.dot_general(a, b, (((1,), (1,)), ((), ())), preferred_element_type=F32)


def _dot_tn(a, b):
    return lax.dot_general(a, b, (((0,), (0,)), ((), ())), preferred_element_type=F32)


def _split_bf16(x):
    hi = x.astype(BF16)
    lo = (x - hi.astype(F32)).astype(BF16)
    return hi, lo


def _params(semantics):
    return pltpu.CompilerParams(dimension_semantics=semantics, vmem_limit_bytes=VMEM_LIMIT_BYTES)


def _ada_kernel(c_ref, w_ref, b_ref, o_ref):
    c = c_ref[...]
    s_hi, s_lo = _split_bf16(c * jax.nn.sigmoid(c))
    w_hi, w_lo = _split_bf16(w_ref[0])
    o_ref[0] = _dot(s_hi, w_hi) + _dot(s_lo, w_hi) + _dot(s_hi, w_lo) + b_ref[0]


def _ada_table(cc, ada_w, ada_b):
    depth, d, six_d = ada_w.shape
    rows = cc.shape[0]
    tn = six_d // 4
    return pl.pallas_call(
        _ada_kernel,
        grid=(depth, six_d // tn),
        in_specs=[
            pl.BlockSpec((rows, d), lambda i, j: (0, 0)),
            pl.BlockSpec((1, d, tn), lambda i, j: (i, 0, j)),
            pl.BlockSpec((1, 1, tn), lambda i, j: (i, 0, j)),
        ],
        out_specs=pl.BlockSpec((1, rows, tn), lambda i, j: (i, 0, j)),
        out_shape=jax.ShapeDtypeStruct((depth, rows, six_d), F32),
        compiler_params=_params(("parallel", "parallel")),
        name="ada_table",
    )(cc, ada_w, ada_b.reshape(depth, 1, six_d))


def _norm_modulate(hn, g, shift, scale):
    ms = jnp.mean(hn * hn, axis=-1, keepdims=True)
    return (hn * lax.rsqrt(ms + RMS_EPS) * g) * (1.0 + scale) + shift


def _swap_half_segments(x, seg):
    lane = lax.broadcasted_iota(jnp.int32, x.shape, 1)
    first = (lane % seg) < (seg // 2)
    return jnp.where(first, pltpu.roll(x, LANES - seg // 2, 1), pltpu.roll(x, seg // 2, 1))


def _proj_kernel(*refs, mixer, has_f):
    if has_f:
        h_ref, f_ref, pmod_ref, *refs = refs
    else:
        h_ref, *refs = refs
    if mixer == "da":
        mod_ref, g_ref, w_ref, cos_ref, sin_ref, qg_ref, kg_ref, seg_ref, hout_ref, o_ref = refs
    else:
        mod_ref, g_ref, w_ref, cos_ref, sin_ref, hout_ref, o_ref = refs

    hn = h_ref[0]
    if has_f:
        hn = hn + pmod_ref[0, 0, 5:6, :] * f_ref[...]
        hout_ref[0] = hn
    mod = mod_ref[0, 0]
    a = _norm_modulate(hn, g_ref[...], mod[0:1], mod[1:2]).astype(BF16)
    cos = cos_ref[...]
    sin = sin_ref[...]
    n_out = w_ref.shape[1]
    d_model = w_ref.shape[0]
    chunk = 1024
    for cb in range(n_out // chunk):
        y = _dot(a, w_ref[:, cb * chunk:(cb + 1) * chunk])
        if mixer == "da":
            for hb in range(chunk // LANES):
                col = cb * chunk + hb * LANES
                yb = y[:, hb * LANES:(hb + 1) * LANES]
                if col < 2 * d_model:
                    y2_hi, y2_lo = _split_bf16(yb * yb)
                    ms = _dot(y2_hi, seg_ref[...]) + _dot(y2_lo, seg_ref[...])
                    gain = qg_ref[...] if col < d_model else kg_ref[...]
                    yn = yb * lax.rsqrt(ms + RMS_EPS) * gain
                    yb = yn * cos + _swap_half_segments(yn, DA_HEAD_DIM) * sin
                    if col < d_model:
                        yb = yb * DA_SCALE
                o_ref[0, :, col:col + LANES] = yb.astype(BF16)
        else:
            for hb in range(chunk // RT_QK_DIM):
                col = cb * chunk + hb * RT_QK_DIM
                yb = y[:, hb * RT_QK_DIM:(hb + 1) * RT_QK_DIM]
                if col < 2 * RT_HEADS * RT_QK_DIM:
                    x1 = yb[:, :LANES]
                    x2 = yb[:, LANES:]
                    r1 = x1 * cos - x2 * sin
                    r2 = x1 * sin + x2 * cos
                    if col >= RT_HEADS * RT_QK_DIM:
                        r1 = r1 * RT_SCALE
                        r2 = r2 * RT_SCALE
                    o_ref[0, :, col:col + LANES] = r1.astype(BF16)
                    o_ref[0, :, col + LANES:col + 2 * LANES] = r2.astype(BF16)
                else:
                    o_ref[0, :, col:col + RT_QK_DIM] = yb.astype(BF16)


def _norm_proj(h, f, pmod, mod, g, w, cos, sin, extra, *, mixer, n_lat_tiles):
    b, n, d = h.shape
    n_out = w.shape[1]
    tiles = n // ROW_TILE
    has_f = f is not None
    row = lambda i, t: (i, t, 0)
    mod_spec = pl.BlockSpec((1, 1, SUBLANES, d), lambda i, t: (i, t // n_lat_tiles, 0, 0))
    const = lambda shape: pl.BlockSpec(shape, lambda i, t: (0,) * len(shape))
    in_specs = [pl.BlockSpec((1, ROW_TILE, d), row)]
    args = [h]
    if has_f:
        in_specs += [pl.BlockSpec((ROW_TILE, d), lambda i, t: (i * tiles + t, 0)), mod_spec]
        args += [f, pmod]
    in_specs += [
        mod_spec,
        const((1, d)),
        pl.BlockSpec((d, n_out), lambda i, t: (0, 0), pipeline_mode=pl.Buffered(1)),
        pl.BlockSpec((ROW_TILE, LANES), lambda i, t: (t, 0)),
        pl.BlockSpec((ROW_TILE, LANES), lambda i, t: (t, 0)),
    ]
    args += [mod, g.reshape(1, d), w, cos, sin]
    for e in extra:
        in_specs.append(const(e.shape))
        args.append(e)
    out_shape = [jax.ShapeDtypeStruct((b, n, d), F32), jax.ShapeDtypeStruct((b, n, n_out), BF16)]
    out_specs = [pl.BlockSpec((1, ROW_TILE, d), row), pl.BlockSpec((1, ROW_TILE, n_out), row)]
    if not has_f:
        out_shape, out_specs = out_shape[1:], out_specs[1:]
        kernel = functools.partial(_proj_kernel_no_residual, mixer=mixer)
    else:
        kernel = functools.partial(_proj_kernel, mixer=mixer, has_f=True)
    res = pl.pallas_call(
        kernel,
        grid=(b, tiles),
        in_specs=in_specs,
        out_specs=out_specs,
        out_shape=out_shape,
        compiler_params=_params(("parallel", "parallel")),
        name=f"norm_proj_{mixer}",
    )(*args)
    if has_f:
        return res[0], res[1]
    return h, res[0]


def _proj_kernel_no_residual(*refs, mixer):
    *ins, o_ref = refs
    _proj_kernel(*ins, None, o_ref, mixer=mixer, has_f=False)


def _diff_attn_kernel(q_ref, k_ref, v_ref, lam_ref, sg_ref, o_ref, *, n_lat, lam_init):
    t = pl.program_id(2)
    lane = lax.broadcasted_iota(jnp.int32, (1, LANES), 1)
    first = lane < DA_HEAD_DIM
    q = q_ref[0]
    zero = jnp.zeros_like(q)
    q1 = jnp.where(first, q, zero)
    q2 = jnp.where(first, zero, q)
    lam = lam_ref[...]
    lam_full = (jnp.exp(jnp.sum(lam[0:1] * lam[1:2], axis=-1, keepdims=True))
                - jnp.exp(jnp.sum(lam[2:3] * lam[3:4], axis=-1, keepdims=True)) + lam_init)

    def attend(k, v):
        def one_map(qm):
            s = _dot_nt(qm, k)
            p = jnp.exp(s - jnp.max(s, axis=-1, keepdims=True))
            return _dot(p.astype(BF16), v) / jnp.sum(p, axis=-1, keepdims=True)
        a = one_map(q1) - lam_full * one_map(q2)
        ms = jnp.mean(a * a, axis=-1, keepdims=True)
        o_ref[0] = (a * lax.rsqrt(ms + RMS_EPS) * sg_ref[...] * (1.0 - lam_init)).astype(BF16)

    @pl.when(t < n_lat // ROW_TILE)
    def _():
        attend(k_ref[0], v_ref[0])

    @pl.when(t >= n_lat // ROW_TILE)
    def _():
        attend(k_ref[0, n_lat:, :], v_ref[0, n_lat:, :])


def _diff_attention(qkv, lam, subln_g, *, n_lat, lam_init):
    b, n, _ = qkv.shape
    hd = 2 * DA_HEAD_DIM
    return pl.pallas_call(
        functools.partial(_diff_attn_kernel, n_lat=n_lat, lam_init=lam_init),
        grid=(b, DA_HEADS, n // ROW_TILE),
        in_specs=[
            pl.BlockSpec((1, ROW_TILE, hd), lambda i, h, t: (i, t, h)),
            pl.BlockSpec((1, n, hd), lambda i, h, t: (i, 0, DA_HEADS + h)),
            pl.BlockSpec((1, n, hd), lambda i, h, t: (i, 0, 2 * DA_HEADS + h)),
            pl.BlockSpec(lam.shape, lambda i, h, t: (0, 0)),
            pl.BlockSpec((1, hd), lambda i, h, t: (0, 0)),
        ],
        out_specs=pl.BlockSpec((1, ROW_TILE, hd), lambda i, h, t: (i, t, h)),
        out_shape=jax.ShapeDtypeStruct((b, n, DA_HEADS * hd), BF16),
        compiler_params=_params(("parallel", "parallel", "arbitrary")),
        name="diff_attention",
    )(qkv, qkv, qkv, lam, subln_g.reshape(1, hd))


def _retention_kernel(lg_ref, q_ref, k_ref, v_ref, gf_ref, gb_ref, o_ref, sf_ref, sb_ref, yf_ref, yb_ref,
                      *, n_lat):
    head = pl.program_id(1)
    n = q_ref.shape[1]
    c = RT_CHUNK
    n_chunks = n // c
    lat_chunks = n_lat // c
    lg_f = jnp.full((1, 1), lg_ref[0, head], F32)
    lg_b = jnp.full((1, 1), lg_ref[1, head], F32)
    pos_r = lax.broadcasted_iota(jnp.int32, (c, c), 0).astype(F32)
    pos_c = lax.broadcasted_iota(jnp.int32, (c, c), 1).astype(F32)
    rel = pos_r - pos_c
    col = lax.broadcasted_iota(jnp.int32, (c, 1), 0).astype(F32)
    intra_f = jnp.where(rel >= 0, jnp.exp(lg_f * jnp.maximum(rel, 0.0)), 0.0)
    intra_b = jnp.where(rel <= 0, jnp.exp(lg_b * jnp.maximum(-rel, 0.0)), 0.0)
    dq_f = jnp.exp(lg_f * (col + 1.0))
    dk_f = jnp.exp(lg_f * (c - 1.0 - col))
    dq_b = jnp.exp(lg_b * (c - col))
    dk_b = jnp.exp(lg_b * col)
    dc_f = jnp.exp(lg_f * c)
    dc_b = jnp.exp(lg_b * c)
    sf_ref[...] = jnp.zeros_like(sf_ref)
    sb_ref[...] = jnp.zeros_like(sb_ref)

    def one_chunk(idx, s_ref, intra, dq, dk, dc, g_ref, y_ref):
        rows = pl.ds(pl.multiple_of(idx * c, c), c)
        qc = q_ref[0, rows, :]
        kc = k_ref[0, rows, :]
        vc = v_ref[0, rows, :]
        state = s_ref[...]
        scores = (_dot_nt(qc, kc) * intra).astype(BF16)
        y = _dot(scores, vc) + _dot(qc, state.astype(BF16)) * dq
        s_ref[...] = state * dc + _dot_tn((kc.astype(F32) * dk).astype(BF16), vc)
        ms = jnp.mean(y * y, axis=-1, keepdims=True)
        gate = g_ref[0, rows, :].astype(F32)
        y_ref[rows, :] = y * lax.rsqrt(ms + RMS_EPS) * (gate * jax.nn.sigmoid(gate))

    def step(s, carry):
        idx_f = (s + lat_chunks) % n_chunks
        idx_b = n_chunks - 1 - s
        one_chunk(idx_f, sf_ref, intra_f, dq_f, dk_f, dc_f, gf_ref, yf_ref)
        one_chunk(idx_b, sb_ref, intra_b, dq_b, dk_b, dc_b, gb_ref, yb_ref)
        return carry

    lax.fori_loop(0, n_chunks, step, 0)
    o_ref[0] = (yf_ref[...] + yb_ref[...]).astype(BF16)


def _retention(proj, log_g, *, n_lat):
    b, n, _ = proj.shape
    kq = RT_HEADS * RT_QK_DIM // RT_QK_DIM
    v0 = 2 * RT_HEADS * RT_QK_DIM // RT_V_DIM
    return pl.pallas_call(
        functools.partial(_retention_kernel, n_lat=n_lat),
        grid=(b, RT_HEADS),
        in_specs=[
            pl.BlockSpec(memory_space=pltpu.SMEM),
            pl.BlockSpec((1, n, RT_QK_DIM), lambda i, h: (i, 0, h)),
            pl.BlockSpec((1, n, RT_QK_DIM), lambda i, h: (i, 0, kq + h)),
            pl.BlockSpec((1, n, RT_V_DIM), lambda i, h: (i, 0, v0 + h)),
            pl.BlockSpec((1, n, RT_V_DIM), lambda i, h: (i, 0, v0 + RT_HEADS + h)),
            pl.BlockSpec((1, n, RT_V_DIM), lambda i, h: (i, 0, v0 + 2 * RT_HEADS + h)),
        ],
        out_specs=pl.BlockSpec((1, n, RT_V_DIM), lambda i, h: (i, 0, h)),
        out_shape=jax.ShapeDtypeStruct((b, n, RT_HEADS * RT_V_DIM), BF16),
        scratch_shapes=[
            pltpu.VMEM((RT_QK_DIM, RT_V_DIM), F32),
            pltpu.VMEM((RT_QK_DIM, RT_V_DIM), F32),
            pltpu.VMEM((n, RT_V_DIM), F32),
            pltpu.VMEM((n, RT_V_DIM), F32),
        ],
        compiler_params=_params(("parallel", "arbitrary")),
        name="retention",
    )(log_g, proj, proj, proj, proj, proj)


def _route(logits):
    lane = lax.broadcasted_iota(jnp.int32, logits.shape, 1).astype(F32)
    big = float(LANES)

    def top(vals):
        m = jnp.max(vals, axis=-1, keepdims=True)
        idx = jnp.min(jnp.where(vals == m, lane, big), axis=-1, keepdims=True)
        return m, idx

    gl = jnp.where(lane < MOE_GROUPS, logits, NEG_INF)
    gmax, gidx = top(gl)
    p_sel = 1.0 / jnp.sum(jnp.exp(gl - gmax), axis=-1, keepdims=True)
    lo_lane = MOE_GROUPS + MOE_PER_GROUP * gidx
    el = jnp.where((lane >= lo_lane) & (lane < lo_lane + MOE_PER_GROUP), logits, NEG_INF)
    m1, i1 = top(el)
    m2, i2 = top(jnp.where(lane == i1, NEG_INF, el))
    e2 = jnp.exp(m2 - m1)
    w1 = p_sel / (1.0 + e2)
    w2 = p_sel * e2 / (1.0 + e2)
    first_lo = i1 < i2
    e_lo = jnp.where(first_lo, i1, i2) - MOE_GROUPS
    e_hi = jnp.where(first_lo, i2, i1) - MOE_GROUPS
    w_lo = jnp.where(first_lo, w1, w2)
    w_hi = jnp.where(first_lo, w2, w1)
    a = e_lo - MOE_PER_GROUP * gidx
    bb = e_hi - MOE_PER_GROUP * gidx
    pair = a * MOE_PER_GROUP - a * (a + 1.0) * 0.5 + (bb - a - 1.0)
    cls = gidx * MOE_PAIRS + pair
    out = jnp.zeros_like(logits)
    for k, val in enumerate((cls, w_lo, w_hi, e_lo, e_hi)):
        out = jnp.where(lane == k, val, out)
    return out


def _out_proj_kernel(o_ref, h_ref, mod_ref, g_ref, wo_ref, wr_hi_ref, wr_lo_ref, br_ref, hmid_ref, fin_ref):
    d = h_ref.shape[2]
    mod = mod_ref[0, 0]
    hn = h_ref[0] + mod[2:3] * _dot(o_ref[0], wo_ref[...])
    hmid_ref[0] = hn
    fin = _norm_modulate(hn, g_ref[...], mod[3:4], mod[4:5])
    f_hi, f_lo = _split_bf16(fin)
    logits = (_dot(f_hi, wr_hi_ref[...]) + _dot(f_lo, wr_hi_ref[...]) + _dot(f_hi, wr_lo_ref[...])
              + br_ref[...])
    fin_ref[:, :d] = fin
    fin_ref[:, d:] = _route(logits)


def _out_proj(o, h, mod, g, wo, wr_hi, wr_lo, br, *, n_lat_tiles):
    b, n, d = h.shape
    dv = o.shape[2]
    tiles = n // ROW_TILE
    row = lambda i, t: (i, t, 0)
    const = lambda shape: pl.BlockSpec(shape, lambda i, t: (0,) * len(shape))
    return pl.pallas_call(
        _out_proj_kernel,
        grid=(b, tiles),
        in_specs=[
            pl.BlockSpec((1, ROW_TILE, dv), row),
            pl.BlockSpec((1, ROW_TILE, d), row),
            pl.BlockSpec((1, 1, SUBLANES, d), lambda i, t: (i, t // n_lat_tiles, 0, 0)),
            const((1, d)),
            const((dv, d)),
            const((d, LANES)),
            const((d, LANES)),
            const((1, LANES)),
        ],
        out_specs=[
            pl.BlockSpec((1, ROW_TILE, d), row),
            pl.BlockSpec((ROW_TILE, d + META_W), lambda i, t: (i * tiles + t, 0)),
        ],
        out_shape=[
            jax.ShapeDtypeStruct((b, n, d), F32),
            jax.ShapeDtypeStruct((b * n, d + META_W), F32),
        ],
        compiler_params=_params(("parallel", "parallel")),
        name="out_proj_router",
    )(o, h, mod, g.reshape(1, d), wo, wr_hi, wr_lo, br)


def _moe_kernel(tlo_ref, thi_ref, cnt_ref, tok_ref, fin_hbm, wg_lo, wu_lo, wd_lo, wg_hi, wu_hi, wd_hi,
                out_hbm, xbuf, obuf, gsem, ssem):
    i = pl.program_id(0)
    tm, d = obuf.shape
    count = cnt_ref[i]

    def gather(r):
        return pltpu.make_async_copy(fin_hbm.at[pl.ds(tok_ref[0, 0, r], 1), :], xbuf.at[pl.ds(r, 1), :], gsem)

    def scatter(r):
        return pltpu.make_async_copy(obuf.at[pl.ds(r, 1), :], out_hbm.at[pl.ds(tok_ref[0, 0, r], 1), :], ssem)

    def move_rows(copy):
        def loops(n_rows, **kw):
            def start(r, carry):
                copy(r).start()
                return carry
            def wait(r, carry):
                copy(r).wait()
                return carry
            lax.fori_loop(0, n_rows, start, 0, **kw)
            lax.fori_loop(0, n_rows, wait, 0, **kw)

        @pl.when(count == tm)
        def _():
            loops(tm, unroll=8)

        @pl.when(count < tm)
        def _():
            loops(count)

    @pl.when(i == 0)
    def _():
        xbuf[...] = jnp.zeros_like(xbuf)

    @pl.when(count > 0)
    def _():
        move_rows(gather)
        x = xbuf[:, :d].astype(BF16)
        meta = xbuf[:, d:]
        acc = None
        for k, (wg, wu, wd) in enumerate(((wg_lo, wu_lo, wd_lo), (wg_hi, wu_hi, wd_hi))):
            gate = _dot(x, wg[0])
            a = (gate * jax.nn.sigmoid(gate)) * _dot(x, wu[0])
            y = meta[:, 1 + k:2 + k] * _dot(a.astype(BF16), wd[0])
            acc = y if acc is None else acc + y
        obuf[...] = acc
        move_rows(scatter)


def _moe_plan(cls, n_tokens):
    tm = MOE_TILE
    n_tiles_max = n_tokens // tm + MOE_CLASSES
    lo_tab, hi_tab = [], []
    for g in range(MOE_GROUPS):
        for a in range(MOE_PER_GROUP):
            for bb in range(a + 1, MOE_PER_GROUP):
                lo_tab.append(g * MOE_PER_GROUP + a)
                hi_tab.append(g * MOE_PER_GROUP + bb)
    lo_tab = jnp.asarray(np.array(lo_tab, np.int32))
    hi_tab = jnp.asarray(np.array(hi_tab, np.int32))
    i32 = jnp.int32
    class_ids = jnp.arange(MOE_CLASSES, dtype=i32)
    counts = jnp.sum((cls[:, None] == class_ids[None, :]).astype(i32), axis=0)
    tiles_per = (counts + tm - 1) // tm
    tile_cum = jnp.cumsum(tiles_per)
    n_tiles = tile_cum[-1]
    pad_cum = jnp.cumsum(tiles_per * tm - counts)
    pad_ids = jnp.arange(n_tiles_max * tm - n_tokens, dtype=i32)
    pad_cls = jnp.sum((pad_cum[None, :] <= pad_ids[:, None]).astype(i32), axis=1)
    order = jnp.argsort(jnp.concatenate([cls, pad_cls]), stable=True).astype(i32)
    tok = jnp.where(order < n_tokens, order, 0)
    tile_ids = jnp.arange(n_tiles_max, dtype=i32)
    last = jnp.minimum(tile_ids, n_tiles - 1)
    tile_cls = jnp.minimum(jnp.sum((tile_cum[None, :] <= last[:, None]).astype(i32), axis=1), MOE_CLASSES - 1)
    onehot = (tile_cls[:, None] == class_ids[None, :]).astype(i32)
    pick = lambda table: jnp.sum(onehot * table[None, :], axis=1)
    local = tile_ids - pick(tile_cum - tiles_per)
    tile_cnt = jnp.where(tile_ids < n_tiles, jnp.clip(pick(counts) - local * tm, 0, tm), 0)
    return pick(lo_tab), pick(hi_tab), tile_cnt, tok.reshape(n_tiles_max, 1, tm)


def _moe(fin, w_gate, w_up, w_down):
    n_tokens, width = fin.shape
    d = width - META_W
    hidden = w_gate.shape[2]
    tm = MOE_TILE
    cls = fin[:, d].astype(jnp.int32)
    tile_lo, tile_hi, tile_cnt, tok = _moe_plan(cls, n_tokens)
    n_tiles_max = tok.shape[0]
    idx_spec = pl.BlockSpec((1, 1, tm), lambda i, lo, hi, nt: (i, 0, 0), memory_space=pltpu.SMEM)
    w_in = lambda sel: pl.BlockSpec((1, d, hidden), lambda i, lo, hi, nt: ((lo, hi)[sel][i], 0, 0))
    w_out = lambda sel: pl.BlockSpec((1, hidden, d), lambda i, lo, hi, nt: ((lo, hi)[sel][i], 0, 0))
    return pl.pallas_call(
        _moe_kernel,
        grid_spec=pltpu.PrefetchScalarGridSpec(
            num_scalar_prefetch=3,
            grid=(n_tiles_max,),
            in_specs=[
                idx_spec,
                pl.BlockSpec(memory_space=pl.ANY),
                w_in(0), w_in(0), w_out(0), w_in(1), w_in(1), w_out(1),
            ],
            out_specs=pl.BlockSpec(memory_space=pl.ANY),
            scratch_shapes=[
                pltpu.VMEM((tm, width), F32),
                pltpu.VMEM((tm, d), F32),
                pltpu.SemaphoreType.DMA(()),
                pltpu.SemaphoreType.DMA(()),
            ],
        ),
        out_shape=jax.ShapeDtypeStruct((n_tokens, d), F32),
        compiler_params=_params(("arbitrary",)),
        name="moe_experts",
    )(tile_lo, tile_hi, tile_cnt, tok, fin, w_gate, w_up, w_down, w_gate, w_up, w_down)


def _final_kernel(h_ref, f_ref, mod_ref, o_ref):
    o_ref[0] = h_ref[0] + mod_ref[0, 0, 5:6, :] * f_ref[...]


def _final_residual(h, f, mod, *, n_lat):
    b, n, d = h.shape
    tiles = n // ROW_TILE
    lat_tiles = n_lat // ROW_TILE
    return pl.pallas_call(
        _final_kernel,
        grid=(b, lat_tiles),
        in_specs=[
            pl.BlockSpec((1, ROW_TILE, d), lambda i, t: (i, t, 0)),
            pl.BlockSpec((ROW_TILE, d), lambda i, t: (i * tiles + t, 0)),
            pl.BlockSpec((1, 1, SUBLANES, d), lambda i, t: (i, 0, 0, 0)),
        ],
        out_specs=pl.BlockSpec((1, ROW_TILE, d), lambda i, t: (i, t, 0)),
        out_shape=jax.ShapeDtypeStruct((b, n_lat, d), F32),
        compiler_params=_params(("parallel", "parallel")),
        name="final_residual",
    )(h, f, mod)


def _rope_tables(n_lat, n_ctx, head_dim, sign_pattern):
    t = np.arange(n_lat)
    row = (t // GRID_W).astype(np.float32)
    colp = (t % GRID_W).astype(np.float32)
    n_freq = head_dim // 4
    inv_freq = jnp.asarray(ROPE_BASE, F32) ** (-jnp.arange(n_freq, dtype=F32) / n_freq)
    ang = jnp.concatenate([jnp.asarray(row)[:, None] * inv_freq, jnp.asarray(colp)[:, None] * inv_freq], axis=-1)
    cos, sin = jnp.cos(ang), jnp.sin(ang)
    reps = LANES // cos.shape[1]
    cos_t = jnp.tile(cos, (1, reps))
    sin_t = jnp.concatenate([sin * s for s in sign_pattern[:reps]], axis=1) if reps > 1 else sin
    cos_t = jnp.concatenate([cos_t, jnp.ones((n_ctx, LANES), F32)], axis=0)
    sin_t = jnp.concatenate([sin_t, jnp.zeros((n_ctx, LANES), F32)], axis=0)
    return cos_t, sin_t


def kernel(x, c, ctx, c_ctx, norm1_g, norm2_g, ada_w, ada_b, da_w_qkv, da_q_norm_g, da_k_norm_g, da_lambda,
           da_subln_g, da_w_o, rt_w_in, rt_decay, rt_w_o, moe_w_group, moe_b_group, moe_w_expert, moe_b_expert,
           moe_w_gate, moe_w_up, moe_w_down):
    b, n_lat, d = x.shape
    n_ctx = ctx.shape[1]
    depth = ada_w.shape[0]
    assert n_lat % ROW_TILE == 0 and n_ctx == ROW_TILE and d % LANES == 0
    n_lat_tiles = n_lat // ROW_TILE
    n_tokens = b * (n_lat + n_ctx)
    assert n_tokens % MOE_TILE == 0

    rows = -(-(b + 1) // SUBLANES) * SUBLANES
    cc = jnp.concatenate([c, c_ctx[None, :], jnp.zeros((rows - b - 1, d), F32)], axis=0)
    mods = _ada_table(cc, ada_w, ada_b).reshape(depth, rows, 6, d)
    mod_lat = mods[:, :b]
    mod_ctx = jnp.broadcast_to(mods[:, b][:, None], mod_lat.shape)
    modtab = jnp.stack([mod_lat, mod_ctx], axis=2)
    modtab = jnp.pad(modtab, ((0, 0), (0, 0), (0, 0), (0, SUBLANES - 6), (0, 0)))

    cos_da, sin_da = _rope_tables(n_lat, n_ctx, DA_HEAD_DIM, (-1.0, 1.0, -1.0, 1.0))
    cos_rt, sin_rt = _rope_tables(n_lat, n_ctx, RT_QK_DIM, (1.0,))
    seg = np.zeros((LANES, LANES), np.float32)
    for s in range(LANES // DA_HEAD_DIM):
        seg[s * DA_HEAD_DIM:(s + 1) * DA_HEAD_DIM, s * DA_HEAD_DIM:(s + 1) * DA_HEAD_DIM] = 1.0 / DA_HEAD_DIM
    seg = jnp.asarray(seg, BF16)

    h = jnp.concatenate([x, ctx], axis=1)
    f = None
    for i in range(depth):
        j = i // 2
        pmod = modtab[i - 1] if i > 0 else None
        if i % 2 == 0:
            lam_init = 0.8 - 0.6 * math.exp(-0.3 * i)
            reps = LANES // DA_HEAD_DIM
            extra = (jnp.tile(da_q_norm_g[j], reps)[None, :], jnp.tile(da_k_norm_g[j], reps)[None, :], seg)
            h, qkv = _norm_proj(h, f, pmod, modtab[i], norm1_g[i], da_w_qkv[j].astype(BF16), cos_da, sin_da,
                                extra, mixer="da", n_lat_tiles=n_lat_tiles)
            o = _diff_attention(qkv, da_lambda[j], da_subln_g[j], n_lat=n_lat, lam_init=lam_init)
            w_o = da_w_o[j]
        else:
            h, proj = _norm_proj(h, f, pmod, modtab[i], norm1_g[i], rt_w_in[j].astype(BF16), cos_rt, sin_rt,
                                 (), mixer="rt", n_lat_tiles=n_lat_tiles)
            log_g = jnp.log1p(-jnp.exp(rt_decay[j].astype(F32)))
            o = _retention(proj, log_g, n_lat=n_lat)
            w_o = rt_w_o[j]
        w_r = jnp.concatenate([moe_w_group[i], moe_w_expert[i]], axis=1)
        w_r = jnp.pad(w_r, ((0, 0), (0, LANES - w_r.shape[1])))
        b_r = jnp.pad(jnp.concatenate([moe_b_group[i], moe_b_expert[i]]), (0, LANES - MOE_GROUPS - MOE_EXPERTS))
        wr_hi, wr_lo = _split_bf16(w_r)
        h, fin = _out_proj(o, h, modtab[i], norm2_g[i], w_o.astype(BF16), wr_hi, wr_lo, b_r[None, :],
                           n_lat_tiles=n_lat_tiles)
        f = _moe(fin, moe_w_gate[i].astype(BF16), moe_w_up[i].astype(BF16), moe_w_down[i].astype(BF16))
    return _final_residual(h, f, modtab[depth - 1], n_lat=n_lat)
```
